```python
import math
import jax, jax.numpy as jnp
from jax import lax
import numpy as np

D_MODEL = 1024
BATCH = 16
SEQ = 2048
DEPTH = 2
DEC_BATCH = 32
DEC_SEQ = 64
PAST_LEN = 1024

CHUNK = 64
Q_BLOCK = 128
N_EVEN = (DEPTH + 1) // 2
N_ODD = DEPTH // 2
MLA_HEADS = D_MODEL // 128
QK_NOPE = 64
QK_ROPE = 32
V_HEAD = 64
Q_LORA = D_MODEL // 4
KV_LORA = D_MODEL // 4
ROPE_BASE = 10000.0
MLA_OUT = MLA_HEADS * V_HEAD
CONV_CH = D_MODEL // 2
CONV_K = 3
EVEN_IN = Q_LORA + KV_LORA + QK_ROPE + 3 * CONV_CH
EVEN_SPLITS = (Q_LORA, Q_LORA + KV_LORA, Q_LORA + KV_LORA + QK_ROPE,
               Q_LORA + KV_LORA + QK_ROPE + CONV_CH, Q_LORA + KV_LORA + QK_ROPE + 2 * CONV_CH)
SSM_WIDTH = D_MODEL
SSM_GROUP = 16
SSM_GROUPS = SSM_WIDTH // SSM_GROUP
SSM_STATE = 64
MEM_LEN = 256
X_HEADS = 4
X_HEAD_DIM = D_MODEL // X_HEADS
D_FF = 256 * ((8 * D_MODEL // 3 + 255) // 256)
EPS = 1e-6
NEG_INF = -1e30

kernel_name = 'hybrid_mla_conv_s5_streaming_step'


def rms_norm(x, g):
    x32 = x.astype(jnp.float32)
    y = x32 * lax.rsqrt(jnp.mean(x32 * x32, axis=-1, keepdims=True) + EPS)
    return (y * g.astype(jnp.float32)).astype(x.dtype)


def swiglu(x, w_gu, w_down):
    gate, up = jnp.split(x @ w_gu, 2, axis=-1)
    return (jax.nn.silu(gate) * up) @ w_down


def rope_cos_sin(pos):
    inv = ROPE_BASE ** (-jnp.arange(0, QK_ROPE, 2, dtype=jnp.float32) / QK_ROPE)
    ang = pos.astype(jnp.float32)[:, None] * inv[None, :]
    return jnp.cos(ang), jnp.sin(ang)


def apply_rope(x, cos, sin):
    x32 = x.astype(jnp.float32)
    x1, x2 = jnp.split(x32, 2, axis=-1)
    return jnp.concatenate([x1 * cos - x2 * sin, x1 * sin + x2 * cos], axis=-1).astype(x.dtype)


def chunk_causal_attention(q_nope, q_rope, q_pos, k_nope, k_rope, v, k_pos):
    B, Sq, H, _ = q_nope.shape
    scale = (QK_NOPE + QK_ROPE) ** -0.5
    k_chunk = k_pos // CHUNK

    def attend(args):
        qn, qr, qp = args
        s = jnp.einsum('bqhd,bkhd->bhqk', qn, k_nope) + jnp.einsum('bqhd,bkd->bhqk', qr, k_rope)
        s = s.astype(jnp.float32) * scale
        visible = k_chunk[None, :] <= (qp // CHUNK)[:, None]
        s = jnp.where(visible[None, None], s, NEG_INF)
        p = jax.nn.softmax(s, axis=-1).astype(v.dtype)
        return jnp.einsum('bhqk,bkhd->bqhd', p, v)

    if Sq <= Q_BLOCK:
        return attend((q_nope, q_rope, q_pos))
    nb = Sq // Q_BLOCK

    def to_blocks(t):
        return t.reshape((B, nb, Q_BLOCK) + t.shape[2:]).swapaxes(0, 1)

    out = lax.map(attend, (to_blocks(q_nope), to_blocks(q_rope), q_pos.reshape(nb, Q_BLOCK)))
    return out.swapaxes(0, 1).reshape(B, Sq, H, V_HEAD)


def even_mixer(h, pos, w_in, q_norm, kv_norm, w_uq, w_ukv, conv_w, w_out,
               past_latent, past_krope, past_conv):
    B, S, _ = h.shape
    c_q, c_kv, k_r, gate_b, gate_c, v_in = jnp.split(h @ w_in, EVEN_SPLITS, axis=-1)
    cos, sin = rope_cos_sin(pos)
    q = (rms_norm(c_q, q_norm) @ w_uq).reshape(B, S, MLA_HEADS, QK_NOPE + QK_ROPE)
    q_nope = q[..., :QK_NOPE]
    q_rope = apply_rope(q[..., QK_NOPE:], cos[None, :, None], sin[None, :, None])
    latent = rms_norm(c_kv, kv_norm)
    k_rope = apply_rope(k_r, cos[None], sin[None])
    if past_latent is None:
        lat_all, kr_all, k_pos = latent, k_rope, pos
    else:
        lat_all = jnp.concatenate([past_latent, latent], axis=1)
        kr_all = jnp.concatenate([past_krope, k_rope], axis=1)
        k_pos = jnp.arange(past_latent.shape[1] + S)
    Sk = lat_all.shape[1]
    kv = (lat_all @ w_ukv).reshape(B, Sk, MLA_HEADS, QK_NOPE + V_HEAD)
    k_nope, v = kv[..., :QK_NOPE], kv[..., QK_NOPE:]
    attn = chunk_causal_attention(q_nope, q_rope, pos, k_nope, kr_all, v, k_pos).reshape(B, S, MLA_OUT)
    u = gate_c * v_in
    if past_conv is None:
        past_conv = jnp.zeros((B, CONV_K - 1, CONV_CH), u.dtype)
    u_pad = jnp.concatenate([past_conv, u], axis=1)
    conv = sum(conv_w[k] * u_pad[:, k:k + S] for k in range(CONV_K))
    z = gate_b * conv
    out = jnp.concatenate([attn, z], axis=-1) @ w_out
    return out, latent, k_rope, u_pad[:, -(CONV_K - 1):]


def ssm_mixer(h, w_in, a_re, a_im, b_re, b_im, c_re, c_im, log_dt, d_skip, w_glu, h0_re, h0_im):
    B, S, _ = h.shape
    f32 = jnp.float32
    u = h @ w_in
    u32 = u.astype(f32)
    ug = u32.reshape(B, S, SSM_GROUPS, SSM_GROUP)
    lam = lax.complex(a_re.astype(f32), a_im.astype(f32))
    dt = jnp.exp(log_dt.astype(f32))[:, None]
    a_bar = jnp.exp(lam * dt)
    b_bar = ((a_bar - 1.0) / lam)[..., None] * lax.complex(b_re.astype(f32), b_im.astype(f32))
    c_mat = lax.complex(c_re.astype(f32), c_im.astype(f32))
    bu = lax.complex(jnp.einsum('bsgc,gpc->bsgp', ug, b_bar.real),
                     jnp.einsum('bsgc,gpc->bsgp', ug, b_bar.imag))
    if h0_re is None:
        h0 = jnp.zeros((B, SSM_GROUPS, SSM_STATE), jnp.complex64)
    else:
        h0 = lax.complex(h0_re.astype(f32), h0_im.astype(f32))
    L = CHUNK if S % CHUNK == 0 else S
    nb = S // L
    bu_blocks = bu.reshape(B, nb, L, SSM_GROUPS, SSM_STATE).swapaxes(0, 1)

    def combine(e1, e2):
        a1, b1 = e1
        a2, b2 = e2
        return a1 * a2, a2 * b1 + b2

    def block_step(h_prev, bu_blk):
        bu_blk = bu_blk.at[:, 0].add(a_bar * h_prev)
        a_seq = jnp.broadcast_to(a_bar, bu_blk.shape)
        _, hs = lax.associative_scan(combine, (a_seq, bu_blk), axis=1)
        y = jnp.einsum('blgp,gcp->blgc', hs, c_mat).real
        return hs[:, -1], y

    h_last, ys = lax.scan(block_step, h0, bu_blocks)
    y = ys.swapaxes(0, 1).reshape(B, S, SSM_WIDTH) + d_skip.astype(f32) * u32
    val, gate = jnp.split(jax.nn.gelu(y).astype(h.dtype) @ w_glu, 2, axis=-1)
    out = val * jax.nn.sigmoid(gate)
    return out, h_last.real, h_last.imag


def memory_kv(mem, g, w_k, w_v):
    B = mem.shape[0]
    m = rms_norm(mem, g)
    return ((m @ w_k).reshape(B, MEM_LEN, X_HEADS, X_HEAD_DIM),
            (m @ w_v).reshape(B, MEM_LEN, X_HEADS, X_HEAD_DIM))


def cross_attend(h, mem_k, mem_v, w_q, w_o):
    B, S, _ = h.shape
    q = (h @ w_q).reshape(B, S, X_HEADS, X_HEAD_DIM)
    s = jnp.einsum('bshd,bmhd->bhsm', q, mem_k).astype(jnp.float32) * (X_HEAD_DIM ** -0.5)
    p = jax.nn.softmax(s, axis=-1).astype(mem_v.dtype)
    o = jnp.einsum('bhsm,bmhd->bshd', p, mem_v).reshape(B, S, X_HEADS * X_HEAD_DIM)
    return o @ w_o


def setup_inputs(seed: int = 0) -> dict:
    key = jax.random.key(seed)
    ks = iter(jax.random.split(key, 64))

    def nrm(shape, scale):
        return scale * jax.random.normal(next(ks), shape, jnp.float32)

    def gain(shape):
        return 1.0 + nrm(shape, 0.01)

    D = D_MODEL
    n_idx = jnp.arange(SSM_STATE, dtype=jnp.float32)
    ssm_shape = (N_ODD, SSM_GROUPS, SSM_STATE)
    return {
        'x_prompt': nrm((BATCH, SEQ, D), 1.0),
        'x_sample': nrm((DEC_BATCH, DEC_SEQ, D), 1.0),
        'cache_mla_latent': nrm((N_EVEN, DEC_BATCH, PAST_LEN, KV_LORA), 1.0),
        'cache_mla_krope': nrm((N_EVEN, DEC_BATCH, PAST_LEN, QK_ROPE), 1.0),
        'state_conv': nrm((N_EVEN, DEC_BATCH, CONV_K - 1, CONV_CH), 1.0),
        'state_ssm_re': nrm((N_ODD, DEC_BATCH, SSM_GROUPS, SSM_STATE), 0.3),
        'state_ssm_im': nrm((N_ODD, DEC_BATCH, SSM_GROUPS, SSM_STATE), 0.3),
        'cache_mem_k': nrm((DEPTH, DEC_BATCH, MEM_LEN, X_HEADS, X_HEAD_DIM), 1.0),
        'cache_mem_v': nrm((DEPTH, DEC_BATCH, MEM_LEN, X_HEADS, X_HEAD_DIM), 1.0),
        'mem_prompt': nrm((BATCH, MEM_LEN, D), 1.0),
        'ln_ffn1': gain((DEPTH, D)),
        'w_ffn1_gu': nrm((DEPTH, D, 2 * D_FF), D ** -0.5),
        'w_ffn1_down': nrm((DEPTH, D_FF, D), D_FF ** -0.5),
        'ln_mix': gain((DEPTH, D)),
        'w_in_even': nrm((N_EVEN, D, EVEN_IN), D ** -0.5),
        'q_norm': gain((N_EVEN, Q_LORA)),
        'kv_norm': gain((N_EVEN, KV_LORA)),
        'w_uq': nrm((N_EVEN, Q_LORA, MLA_HEADS * (QK_NOPE + QK_ROPE)), Q_LORA ** -0.5),
        'w_ukv': nrm((N_EVEN, KV_LORA, MLA_HEADS * (QK_NOPE + V_HEAD)), KV_LORA ** -0.5),
        'conv_w': nrm((N_EVEN, CONV_K, CONV_CH), CONV_K ** -0.5),
        'w_out_even': nrm((N_EVEN, MLA_OUT + CONV_CH, D), (MLA_OUT + CONV_CH) ** -0.5),
        'w_in_odd': nrm((N_ODD, D, SSM_WIDTH), D ** -0.5),
        'ssm_a_re': -0.5 + nrm(ssm_shape, 0.01),
        'ssm_a_im': math.pi * n_idx + nrm(ssm_shape, 0.01),
        'ssm_b_re': nrm((N_ODD, SSM_GROUPS, SSM_STATE, SSM_GROUP), (2 * SSM_GROUP) ** -0.5),
        'ssm_b_im': nrm((N_ODD, SSM_GROUPS, SSM_STATE, SSM_GROUP), (2 * SSM_GROUP) ** -0.5),
        'ssm_c_re': nrm((N_ODD, SSM_GROUPS, SSM_GROUP, SSM_STATE), (2 * SSM_STATE) ** -0.5),
        'ssm_c_im': nrm((N_ODD, SSM_GROUPS, SSM_GROUP, SSM_STATE), (2 * SSM_STATE) ** -0.5),
        'ssm_log_dt': jax.random.uniform(next(ks), (N_ODD, SSM_GROUPS), jnp.float32,
                                         math.log(1e-3), math.log(1e-1)),
        'ssm_d': nrm((N_ODD, SSM_WIDTH), 1.0),
        'w_glu': nrm((N_ODD, SSM_WIDTH, 2 * D), SSM_WIDTH ** -0.5),
        'ln_cross': gain((DEPTH, D)),
        'ln_mem': gain((DEPTH, D)),
        'w_xq': nrm((DEPTH, D, X_HEADS * X_HEAD_DIM), D ** -0.5),
        'w_xk': nrm((DEPTH, D, X_HEADS * X_HEAD_DIM), D ** -0.5),
        'w_xv': nrm((DEPTH, D, X_HEADS * X_HEAD_DIM), D ** -0.5),
        'w_xo': nrm((DEPTH, X_HEADS * X_HEAD_DIM, D), (X_HEADS * X_HEAD_DIM) ** -0.5),
        'ln_ffn2': gain((DEPTH, D)),
        'w_ffn2_gu': nrm((DEPTH, D, 2 * D_FF), D ** -0.5),
        'w_ffn2_down': nrm((DEPTH, D_FF, D), D_FF ** -0.5),
        'ln_final': gain((D,)),
    }


def reference(x_prompt, x_sample, cache_mla_latent, cache_mla_krope, state_conv, state_ssm_re, state_ssm_im,
              cache_mem_k, cache_mem_v, mem_prompt,
              ln_ffn1, w_ffn1_gu, w_ffn1_down, ln_mix, w_in_even, q_norm, kv_norm, w_uq, w_ukv, conv_w,
              w_out_even, w_in_odd, ssm_a_re, ssm_a_im, ssm_b_re, ssm_b_im, ssm_c_re, ssm_c_im, ssm_log_dt,
              ssm_d, w_glu, ln_cross, ln_mem, w_xq, w_xk, w_xv, w_xo, ln_ffn2, w_ffn2_gu, w_ffn2_down,
              ln_final):

    def run(x, past, mem_kv):
        B, S, _ = x.shape
        past_len = 0 if past is None else past[0].shape[2]
        pos = past_len + jnp.arange(S)
        lat_new, kr_new, conv_new, sre_new, sim_new = [], [], [], [], []
        for l in range(DEPTH):
            i = l // 2
            x = x + 0.5 * swiglu(rms_norm(x, ln_ffn1[l]), w_ffn1_gu[l], w_ffn1_down[l])
            h = rms_norm(x, ln_mix[l])
            if l % 2 == 0:
                p_lat = p_kr = p_conv = None
                if past is not None:
                    p_lat, p_kr, p_conv = past[0][i], past[1][i], past[2][i]
                mix, lat, kr, cst = even_mixer(h, pos, w_in_even[i], q_norm[i], kv_norm[i], w_uq[i], w_ukv[i],
                                               conv_w[i], w_out_even[i], p_lat, p_kr, p_conv)
                lat_new.append(lat)
                kr_new.append(kr)
                conv_new.append(cst)
            else:
                h0r = h0i = None
                if past is not None:
                    h0r, h0i = past[3][i], past[4][i]
                mix, sre, sim = ssm_mixer(h, w_in_odd[i], ssm_a_re[i], ssm_a_im[i], ssm_b_re[i], ssm_b_im[i],
                                          ssm_c_re[i], ssm_c_im[i], ssm_log_dt[i], ssm_d[i], w_glu[i], h0r, h0i)
                sre_new.append(sre)
                sim_new.append(sim)
            x = x + mix
            mk, mv = mem_kv[l]
            x = x + cross_attend(rms_norm(x, ln_cross[l]), mk, mv, w_xq[l], w_xo[l])
            x = x + 0.5 * swiglu(rms_norm(x, ln_ffn2[l]), w_ffn2_gu[l], w_ffn2_down[l])
        y = rms_norm(x, ln_final)
        return (y, jnp.stack(lat_new), jnp.stack(kr_new), jnp.stack(conv_new),
                jnp.stack(sre_new), jnp.stack(sim_new))

    mem_p = [memory_kv(mem_prompt, ln_mem[l], w_xk[l], w_xv[l]) for l in range(DEPTH)]
    mem_k_p = jnp.stack([kv[0] for kv in mem_p])
    mem_v_p = jnp.stack([kv[1] for kv in mem_p])
    y_prompt, lat_p, kr_p, conv_p, sre_p, sim_p = run(x_prompt, None, mem_p)

    mem_s = [(cache_mem_k[l], cache_mem_v[l]) for l in range(DEPTH)]
    past = (cache_mla_latent, cache_mla_krope, state_conv, state_ssm_re, state_ssm_im)
    y_sample, lat_s, kr_s, conv_s, sre_s, sim_s = run(x_sample, past, mem_s)

    return (y_prompt, y_sample, lat_p, kr_p, conv_p, sre_p, sim_p, mem_k_p, mem_v_p,
            lat_s, kr_s, conv_s, sre_s, sim_s)
```

```python
import functools
import math

import jax
import jax.numpy as jnp
from jax import lax
from jax.experimental import pallas as pl
from jax.experimental.pallas import tpu as pltpu

F32 = jnp.float32
BF16 = jnp.bfloat16

EPS = 1e-6
NEG_INF = -1e30
CHUNK = 64
QK_NOPE = 64
QK_ROPE = 32
V_HEAD = 64
HEAD_SLOT = 128
ROPE_BASE = 10000.0
CONV_K = 3
SSM_GROUP = 16
SSM_STATE = 64
SSM_GROUPS_PER_BLOCK = 16
X_HEADS = 4
SUBLANES = 8
VMEM_LIMIT_BYTES = 56 * 1024 * 1024


def _params(*sem):
    return pltpu.CompilerParams(dimension_semantics=sem, vmem_limit_bytes=VMEM_LIMIT_BYTES)


def _rms(x, g):
    return x * lax.rsqrt(jnp.mean(x * x, axis=-1, keepdims=True) + EPS) * g


def _dot(a, b):
    return jnp.dot(a, b, preferred_element_type=F32)


def _dot_nt(a, b):
    return lax.dot_general(a, b, (((1,), (1,)), ((), ())), preferred_element_type=F32)


def _resident(shape):
    nd = len(shape)
    return pl.BlockSpec(shape, lambda *_: (0,) * nd, pipeline_mode=pl.Buffered(1))


def _ffn_body(*refs, final):
    if final:
        x_ref, g_ref, wg_ref, wu_ref, wd_ref, gf_ref, o_ref = refs
    else:
        x_ref, g_ref, wg_ref, wu_ref, wd_ref, o_ref = refs
    x = x_ref[...]
    h = _rms(x, g_ref[...]).astype(BF16)
    gate = _dot(h, wg_ref[...])
    up = _dot(h, wu_ref[...])
    a = (gate * jax.nn.sigmoid(gate) * up).astype(BF16)
    y = x + 0.5 * _dot(a, wd_ref[...])
    if final:
        y = _rms(y, gf_ref[...])
    o_ref[...] = y


def _ffn(x, g, wg, wu, wd, g_final=None, *, tm=512):
    T, D = x.shape
    F = wg.shape[1]
    tm = min(tm, T)
    final = g_final is not None
    row = pl.BlockSpec((tm, D), lambda i: (i, 0))
    in_specs = [row, _resident((1, D)), _resident((D, F)), _resident((D, F)), _resident((F, D))]
    args = [x, g, wg, wu, wd]
    if final:
        in_specs.append(_resident((1, D)))
        args.append(g_final)
    return pl.pallas_call(
        functools.partial(_ffn_body, final=final),
        out_shape=jax.ShapeDtypeStruct((T, D), F32),
        grid=(T // tm,),
        in_specs=in_specs,
        out_specs=row,
        compiler_params=_params("parallel"),
        name="ffn_final" if final else "ffn",
    )(*args)


def _memkv_body(m_ref, g_ref, wk_ref, wv_ref, k_ref, v_ref):
    m = _rms(m_ref[...], g_ref[...]).astype(BF16)
    k_ref[...] = _dot(m, wk_ref[...])
    v_ref[...] = _dot(m, wv_ref[...])


def _memory_kv(mem, g, wk, wv, *, tm=512):
    R, D = mem.shape
    L = g.shape[0]
    out = jax.ShapeDtypeStruct((L, R, D), F32)
    wspec = pl.BlockSpec((None, D, D), lambda l, i: (l, 0, 0))
    ospec = pl.BlockSpec((None, tm, D), lambda l, i: (l, i, 0))
    return pl.pallas_call(
        _memkv_body,
        out_shape=(out, out),
        grid=(L, R // tm),
        in_specs=[pl.BlockSpec((tm, D), lambda l, i: (i, 0)),
                  pl.BlockSpec((None, 1, D), lambda l, i: (l, 0, 0)), wspec, wspec],
        out_specs=(ospec, ospec),
        compiler_params=_params("parallel", "parallel"),
        name="memory_kv",
    )(mem, g, wk, wv)


def _even_in_body(x_ref, g_ref, win_ref, qn_ref, kvn_ref, wuq_ref, wukv_ref, cw_ref, rope_ref, pconv_ref,
                  q_ref, k_ref, kv_ref, lat_ref, kr_ref, z_ref, nconv_ref, ubuf, *, bt, tm):
    t = pl.program_id(1)
    D = x_ref.shape[-1]
    R = bt * tm
    ql = qn_ref.shape[-1]
    kvl = kvn_ref.shape[-1]
    cc = cw_ref.shape[-1]
    nslot = k_ref.shape[-1]

    x = x_ref[...].reshape(R, D)
    h = _rms(x, g_ref[...]).astype(BF16)
    p = _dot(h, win_ref[...])
    o = 0
    c_q = p[:, o:o + ql]; o += ql
    c_kv = p[:, o:o + kvl]; o += kvl
    gate_b = p[:, o:o + cc]; o += cc
    gate_c = p[:, o:o + cc]; o += cc
    v_in = p[:, o:o + cc]; o += cc
    k_slot = p[:, o:o + HEAD_SLOT]; o += HEAD_SLOT
    k_slot_sw = p[:, o:o + HEAD_SLOT]

    rope = rope_ref[...]
    reps = nslot // HEAD_SLOT
    cos_q = jnp.tile(rope[:, 0:HEAD_SLOT], (1, reps))[None]
    sin_q = jnp.tile(rope[:, HEAD_SLOT:2 * HEAD_SLOT], (1, reps))[None]
    cos_k = rope[:, 2 * HEAD_SLOT:3 * HEAD_SLOT][None]
    sin_k = rope[:, 3 * HEAD_SLOT:4 * HEAD_SLOT][None]

    q2 = _dot(_rms(c_q, qn_ref[...]).astype(BF16), wuq_ref[...])
    q_rot = q2[:, :nslot].reshape(bt, tm, nslot) * cos_q + q2[:, nslot:].reshape(bt, tm, nslot) * sin_q
    q_ref[...] = q_rot.astype(BF16)

    latent = _rms(c_kv, kvn_ref[...])
    lat_ref[...] = latent.reshape(bt, tm, kvl)
    kvu = _dot(latent.astype(BF16), wukv_ref[...])
    kv_ref[...] = kvu.reshape(bt, tm, nslot).astype(BF16)

    k_rot = k_slot.reshape(bt, tm, HEAD_SLOT) * cos_k + k_slot_sw.reshape(bt, tm, HEAD_SLOT) * sin_k
    kr_ref[...] = k_rot[:, :, QK_NOPE:QK_NOPE + QK_ROPE]
    lane = lax.broadcasted_iota(jnp.int32, (1, nslot), 1)
    is_nope = (lane % HEAD_SLOT) < QK_NOPE
    k_all = jnp.where(is_nope, kvu, 0.0).reshape(bt, tm, nslot) + jnp.tile(k_rot, (1, 1, reps))
    k_ref[...] = k_all.astype(BF16)

    u = (gate_c * v_in).reshape(bt, tm, cc)
    ubuf[:, pl.ds(SUBLANES, tm), :] = u

    @pl.when(t == 0)
    def _():
        ubuf[:, pl.ds(SUBLANES - (CONV_K - 1), CONV_K - 1), :] = pconv_ref[...]

    cw = cw_ref[...]
    conv = (cw[0:1][None] * ubuf[:, pl.ds(SUBLANES - 2, tm), :]
            + cw[1:2][None] * ubuf[:, pl.ds(SUBLANES - 1, tm), :]
            + cw[2:3][None] * u)
    z_ref[...] = (gate_b.reshape(bt, tm, cc) * conv).astype(BF16)
    nconv_ref[...] = u[:, tm - (CONV_K - 1):, :]
    ubuf[:, pl.ds(0, SUBLANES), :] = ubuf[:, pl.ds(tm, SUBLANES), :]


def _even_in(x, g, win, qn, kvn, wuq, wukv, cw, rope, pconv, *, bt, tm):
    B, S, D = x.shape
    nslot = wukv.shape[1]
    kvl = kvn.shape[-1]
    cc = cw.shape[-1]

    def act(w, dt):
        return jax.ShapeDtypeStruct((B, S, w), dt), pl.BlockSpec((bt, tm, w), lambda b, t: (b, t, 0))

    outs = [act(nslot, BF16), act(nslot, BF16), act(nslot, BF16), act(kvl, F32), act(QK_ROPE, F32), act(cc, BF16),
            (jax.ShapeDtypeStruct((B, CONV_K - 1, cc), F32), pl.BlockSpec((bt, CONV_K - 1, cc), lambda b, t: (b, 0, 0)))]
    return pl.pallas_call(
        functools.partial(_even_in_body, bt=bt, tm=tm),
        out_shape=tuple(o[0] for o in outs),
        grid=(B // bt, S // tm),
        in_specs=[pl.BlockSpec((bt, tm, D), lambda b, t: (b, t, 0)),
                  _resident(g.shape), _resident(win.shape), _resident(qn.shape), _resident(kvn.shape),
                  _resident(wuq.shape), _resident(wukv.shape), _resident(cw.shape),
                  pl.BlockSpec((tm, rope.shape[1]), lambda b, t: (t, 0)),
                  pl.BlockSpec((bt, CONV_K - 1, cc), lambda b, t: (b, 0, 0))],
        out_specs=tuple(o[1] for o in outs),
        scratch_shapes=[pltpu.VMEM((bt, tm + SUBLANES, cc), F32)],
        compiler_params=_params("parallel", "arbitrary"),
        name="even_in",
    )(x, g, win, qn, kvn, wuq, wukv, cw, rope, pconv)


def _past_kv_body(lat_ref, kr_ref, wukv_ref, place_ref, k_ref, kv_ref):
    kvu = _dot(lat_ref[...].astype(BF16), wukv_ref[...])
    kv_ref[...] = kvu.astype(BF16)
    lane = lax.broadcasted_iota(jnp.int32, (1, kvu.shape[1]), 1)
    is_nope = (lane % HEAD_SLOT) < QK_NOPE
    k_ref[...] = (jnp.where(is_nope, kvu, 0.0) + _dot(kr_ref[...].astype(BF16), place_ref[...])).astype(BF16)


def _past_kv(lat, kr, wukv, place, *, tm=512):
    R, kvl = lat.shape
    nslot = wukv.shape[1]
    out = jax.ShapeDtypeStruct((R, nslot), BF16)
    ospec = pl.BlockSpec((tm, nslot), lambda i: (i, 0))
    return pl.pallas_call(
        _past_kv_body,
        out_shape=(out, out),
        grid=(R // tm,),
        in_specs=[pl.BlockSpec((tm, kvl), lambda i: (i, 0)), pl.BlockSpec((tm, QK_ROPE), lambda i: (i, 0)),
                  _resident(wukv.shape), _resident(place.shape)],
        out_specs=(ospec, ospec),
        compiler_params=_params("parallel"),
        name="past_kv",
    )(lat, kr, wukv, place)


def _attn_body(q_ref, kf_ref, vf_ref, kt_ref, vt_ref, o_ref, *, tq, tk, n_full_base, heads):
    i = pl.program_id(1)
    n_full = n_full_base + i * (tq // tk)
    row_chunk = lax.broadcasted_iota(jnp.int32, (tq, tq), 0) // CHUNK
    col_chunk = lax.broadcasted_iota(jnp.int32, (tq, tq), 1) // CHUNK
    visible = col_chunk <= row_chunk
    outs = []
    for h in range(heads):
        sl = slice(h * HEAD_SLOT, (h + 1) * HEAD_SLOT)
        qh = q_ref[:, sl]

        def update(carry, s, vh):
            m, l, acc = carry
            m_new = jnp.maximum(m, jnp.max(s, axis=-1, keepdims=True))
            alpha = jnp.exp(m - m_new)
            p = jnp.exp(s - m_new)
            l = alpha * l + jnp.sum(p, axis=-1, keepdims=True)
            acc = alpha * acc + _dot(p.astype(BF16), vh)
            return m_new, l, acc

        def step(j, carry, sl=sl, qh=qh):
            r0 = pl.multiple_of(j * tk, tk)
            return update(carry, _dot_nt(qh, kf_ref[pl.ds(r0, tk), sl]), vf_ref[pl.ds(r0, tk), sl])

        carry = (jnp.full((tq, 1), NEG_INF, F32), jnp.zeros((tq, 1), F32), jnp.zeros((tq, HEAD_SLOT), F32))
        carry = lax.fori_loop(0, n_full, step, carry)
        s = jnp.where(visible, _dot_nt(qh, kt_ref[:, sl]), NEG_INF)
        _, l, acc = update(carry, s, vt_ref[:, sl])
        outs.append(acc / l)
    o_ref[...] = jnp.concatenate(outs, axis=1).astype(BF16)


def _attention(q, k, kv, k_past=None, kv_past=None, *, tq, tk):
    B, S, W = q.shape
    heads = W // HEAD_SLOT
    if k_past is None:
        k_full, v_full, n_full_base = k, kv, 0
    else:
        k_full, v_full, n_full_base = k_past, kv_past, k_past.shape[1] // tk
    Sf = k_full.shape[1]
    tile = pl.BlockSpec((None, tq, W), lambda b, i: (b, i, 0))
    full = pl.BlockSpec((None, Sf, W), lambda b, i: (b, 0, 0))
    return pl.pallas_call(
        functools.partial(_attn_body, tq=tq, tk=tk, n_full_base=n_full_base, heads=heads),
        out_shape=jax.ShapeDtypeStruct((B, S, W), BF16),
        grid=(B, S // tq),
        in_specs=[tile, full, full, tile, tile],
        out_specs=tile,
        compiler_params=_params("parallel", "parallel"),
        name="attention",
    )(q, k_full, v_full, k, kv)


def _cross_body(*refs, bt, tm, pre):
    if pre:
        x_ref, at_ref, z_ref, wa_ref, wz_ref, g_ref, wq_ref, mk_ref, mv_ref, wo_ref, o_ref, oh_ref = refs
    else:
        x_ref, g_ref, wq_ref, mk_ref, mv_ref, wo_ref, o_ref, oh_ref = refs
    D = x_ref.shape[-1]
    R = bt * tm
    x = x_ref[...].reshape(R, D)
    if pre:
        x = (x + _dot(at_ref[...].reshape(R, at_ref.shape[-1]), wa_ref[...])
             + _dot(z_ref[...].reshape(R, z_ref.shape[-1]), wz_ref[...]))
    h = _rms(x, g_ref[...]).astype(BF16)
    q = _dot(h, wq_ref[...]).astype(BF16)
    hd = D // X_HEADS
    scale = hd ** -0.5
    for b in range(bt):
        rows = slice(b * tm, (b + 1) * tm)
        for hh in range(X_HEADS):
            cols = slice(hh * hd, (hh + 1) * hd)
            s = _dot_nt(q[rows, cols], mk_ref[b, :, cols].astype(BF16)) * scale
            e = jnp.exp(s - jnp.max(s, axis=-1, keepdims=True))
            p = e / jnp.sum(e, axis=-1, keepdims=True)
            oh_ref[rows, cols] = _dot(p.astype(BF16), mv_ref[b, :, cols].astype(BF16)).astype(BF16)
    o_ref[...] = (x + _dot(oh_ref[...], wo_ref[...])).reshape(bt, tm, D)


def _cross(x, g, wq, mk, mv, wo, pre=None, *, bt, tm):
    B, S, D = x.shape
    M = mk.shape[1]
    xs = pl.BlockSpec((bt, tm, D), lambda b, t: (b, t, 0))
    ms = pl.BlockSpec((bt, M, D), lambda b, t: (b, 0, 0))
    in_specs, args = [xs], [x]
    if pre is not None:
        at, z, wa, wz = pre
        in_specs += [pl.BlockSpec((bt, tm, at.shape[-1]), lambda b, t: (b, t, 0)),
                     pl.BlockSpec((bt, tm, z.shape[-1]), lambda b, t: (b, t, 0)),
                     _resident(wa.shape), _resident(wz.shape)]
        args += [at, z, wa, wz]
    in_specs += [_resident(g.shape), _resident(wq.shape), ms, ms, _resident(wo.shape)]
    args += [g, wq, mk, mv, wo]
    return pl.pallas_call(
        functools.partial(_cross_body, bt=bt, tm=tm, pre=pre is not None),
        out_shape=jax.ShapeDtypeStruct((B, S, D), F32),
        grid=(B // bt, S // tm),
        in_specs=in_specs,
        out_specs=xs,
        scratch_shapes=[pltpu.VMEM((bt * tm, D), BF16)],
        compiler_params=_params("parallel", "parallel"),
        name="mix_out_cross" if pre is not None else "cross",
    )(*args)


def _ssm_body(x_ref, g_ref, win_ref, bblk_ref, a_ref, cblk_ref, d_ref, wglu_ref, h0r_ref, h0i_ref,
              o_ref, hlr_ref, hli_ref, st_re, st_im, xt, bu, hs, y, *, lt):
    tt = pl.program_id(1)
    D = x_ref.shape[-1]
    nb = bblk_ref.shape[0]
    gw = bblk_ref.shape[1]
    sw = bblk_ref.shape[2] // 2

    @pl.when(tt == 0)
    def _():
        st_re[...] = h0r_ref[...]
        st_im[...] = h0i_ref[...]

    for t in range(lt):
        xt[pl.ds(t * SUBLANES, SUBLANES), :] = x_ref[:, t, :]
    x = xt[...]
    h = _rms(x, g_ref[...]).astype(BF16)
    u = _dot(h, win_ref[...])
    ub = u.astype(BF16)
    for gb in range(nb):
        bu[...] = _dot(ub[:, gb * gw:(gb + 1) * gw], bblk_ref[gb])
        a_re = jnp.broadcast_to(a_ref[gb, 0], (SUBLANES, sw))
        a_im = jnp.broadcast_to(a_ref[gb, 1], (SUBLANES, sw))
        cols = slice(gb * sw, (gb + 1) * sw)

        def step(t, carry, a_re=a_re, a_im=a_im):
            hr, hi = carry
            r0 = pl.multiple_of(t * SUBLANES, SUBLANES)
            nr = a_re * hr - a_im * hi + bu[pl.ds(r0, SUBLANES), 0:sw]
            ni = a_re * hi + a_im * hr + bu[pl.ds(r0, SUBLANES), sw:2 * sw]
            hs[pl.ds(r0, SUBLANES), 0:sw] = nr
            hs[pl.ds(r0, SUBLANES), sw:2 * sw] = ni
            return nr, ni

        hr, hi = lax.fori_loop(0, lt, step, (st_re[:, cols], st_im[:, cols]))
        st_re[:, cols] = hr
        st_im[:, cols] = hi
        y[:, gb * gw:(gb + 1) * gw] = _dot(hs[...].astype(BF16), cblk_ref[gb])

    yy = y[...] + d_ref[...] * u
    vg = _dot(jax.nn.gelu(yy).astype(BF16), wglu_ref[...])
    xt[...] = x + vg[:, :D] * jax.nn.sigmoid(vg[:, D:])
    for t in range(lt):
        o_ref[:, t, :] = xt[pl.ds(t * SUBLANES, SUBLANES), :]
    hlr_ref[...] = st_re[...]
    hli_ref[...] = st_im[...]


def _ssm(x, g, win, bblk, a, cblk, d, wglu, h0r, h0i, *, lt=CHUNK):
    B, S, D = x.shape
    NS = h0r.shape[1]
    R = SUBLANES * lt
    sw2 = bblk.shape[2]
    xs = pl.BlockSpec((SUBLANES, lt, D), lambda b, t: (b, t, 0))
    ss = pl.BlockSpec((SUBLANES, NS), lambda b, t: (b, 0))
    st = jax.ShapeDtypeStruct((B, NS), F32)
    return pl.pallas_call(
        functools.partial(_ssm_body, lt=lt),
        out_shape=(jax.ShapeDtypeStruct((B, S, D), F32), st, st),
        grid=(B // SUBLANES, S // lt),
        in_specs=[xs, _resident(g.shape), _resident(win.shape), _resident(bblk.shape), _resident(a.shape),
                  _resident(cblk.shape), _resident(d.shape), _resident(wglu.shape), ss, ss],
        out_specs=(xs, ss, ss),
        scratch_shapes=[pltpu.VMEM((SUBLANES, NS), F32), pltpu.VMEM((SUBLANES, NS), F32),
                        pltpu.VMEM((R, D), F32), pltpu.VMEM((R, sw2), F32), pltpu.VMEM((R, sw2), F32),
                        pltpu.VMEM((R, D), F32)],
        compiler_params=_params("parallel", "arbitrary"),
        name="ssm",
    )(x, g, win, bblk, a, cblk, d, wglu, h0r, h0i)


def _rope_swap(w):
    half = w.shape[-1] // 2
    return jnp.concatenate([-w[..., half:], w[..., :half]], axis=-1)


def _in_slot(w):
    return jnp.pad(w, ((0, 0), (QK_NOPE, HEAD_SLOT - QK_NOPE - QK_ROPE)))


def _prep_even(w_in, w_uq, w_ukv, w_out, d_model):
    ql = kvl = d_model // 4
    cc = d_model // 2
    heads = d_model // 128
    o = 0
    w_cq = w_in[:, o:o + ql]; o += ql
    w_ckv = w_in[:, o:o + kvl]; o += kvl
    w_kr = w_in[:, o:o + QK_ROPE]; o += QK_ROPE
    w_gb = w_in[:, o:o + cc]; o += cc
    w_gc = w_in[:, o:o + cc]; o += cc
    w_v = w_in[:, o:o + cc]
    win = jnp.concatenate([w_cq, w_ckv, w_gb, w_gc, w_v, _in_slot(w_kr), _in_slot(_rope_swap(w_kr))], axis=1)

    wq = w_uq.reshape(ql, heads, QK_NOPE + QK_ROPE)
    w_nope, w_rope = wq[..., :QK_NOPE], wq[..., QK_NOPE:]
    pad = jnp.zeros((ql, heads, HEAD_SLOT - QK_NOPE - QK_ROPE), w_uq.dtype)
    q_plain = jnp.concatenate([w_nope, w_rope, pad], axis=-1).reshape(ql, heads * HEAD_SLOT)
    q_swap = jnp.concatenate([jnp.zeros_like(w_nope), _rope_swap(w_rope), pad], axis=-1).reshape(ql, heads * HEAD_SLOT)
    wuq = jnp.concatenate([q_plain, q_swap], axis=1)

    wa = w_out[:heads * V_HEAD].reshape(heads, V_HEAD, d_model)
    wa = jnp.concatenate([jnp.zeros_like(wa), wa], axis=1).reshape(heads * HEAD_SLOT, d_model)
    wz = w_out[heads * V_HEAD:]
    place = jnp.tile(_in_slot(jnp.eye(QK_ROPE, dtype=F32)), (1, heads))
    return (win.astype(BF16), wuq.astype(BF16), w_ukv.astype(BF16), wa.astype(BF16), wz.astype(BF16),
            place.astype(BF16))


def _rope_table(pos):
    scale = (QK_NOPE + QK_ROPE) ** -0.5
    inv = ROPE_BASE ** (-jnp.arange(0, QK_ROPE, 2, dtype=F32) / QK_ROPE)
    ang = pos.astype(F32)[:, None] * inv[None, :]
    cos2 = jnp.tile(jnp.cos(ang), (1, 2))
    sin2 = jnp.tile(jnp.sin(ang), (1, 2))
    S = pos.shape[0]
    tail = jnp.zeros((S, HEAD_SLOT - QK_NOPE - QK_ROPE), F32)
    head0 = jnp.zeros((S, QK_NOPE), F32)
    cos_k = jnp.concatenate([head0, cos2, tail], axis=1)
    sin_k = jnp.concatenate([head0, sin2, tail], axis=1)
    cos_q = jnp.concatenate([jnp.ones((S, QK_NOPE), F32), cos2, tail], axis=1) * scale
    return jnp.concatenate([cos_q, sin_k * scale, cos_k, sin_k], axis=1)


def _prep_ssm(a_re, a_im, b_re, b_im, c_re, c_im, log_dt):
    G, P = a_re.shape
    gpb = SSM_GROUPS_PER_BLOCK
    nb = G // gpb
    dt = jnp.exp(log_dt)[:, None]
    mag = jnp.exp(a_re * dt)
    ab_re = mag * jnp.cos(a_im * dt)
    ab_im = mag * jnp.sin(a_im * dt)
    den = a_re * a_re + a_im * a_im
    q_re = ((ab_re - 1.0) * a_re + ab_im * a_im) / den
    q_im = (ab_im * a_re - (ab_re - 1.0) * a_im) / den
    bb_re = q_re[..., None] * b_re - q_im[..., None] * b_im
    bb_im = q_re[..., None] * b_im + q_im[..., None] * b_re
    eye = jnp.eye(gpb, dtype=F32)

    def b_block(bb):
        bb = bb.reshape(nb, gpb, P, SSM_GROUP)
        return jnp.einsum("ngpc,gh->ngchp", bb, eye).reshape(nb, gpb * SSM_GROUP, gpb * P)

    def c_block(c):
        c = c.reshape(nb, gpb, SSM_GROUP, P)
        return jnp.einsum("ngcp,gh->ngphc", c, eye).reshape(nb, gpb * P, gpb * SSM_GROUP)

    bblk = jnp.concatenate([b_block(bb_re), b_block(bb_im)], axis=2)
    cblk = jnp.concatenate([c_block(c_re), c_block(-c_im)], axis=1)
    a = jnp.stack([ab_re.reshape(nb, 1, gpb * P), ab_im.reshape(nb, 1, gpb * P)], axis=1)
    return bblk.astype(BF16), cblk.astype(BF16), a


def kernel(x_prompt, x_sample, cache_mla_latent, cache_mla_krope, state_conv, state_ssm_re, state_ssm_im,
           cache_mem_k, cache_mem_v, mem_prompt,
           ln_ffn1, w_ffn1_gu, w_ffn1_down, ln_mix, w_in_even, q_norm, kv_norm, w_uq, w_ukv, conv_w,
           w_out_even, w_in_odd, ssm_a_re, ssm_a_im, ssm_b_re, ssm_b_im, ssm_c_re, ssm_c_im, ssm_log_dt,
           ssm_d, w_glu, ln_cross, ln_mem, w_xq, w_xk, w_xv, w_xo, ln_ffn2, w_ffn2_gu, w_ffn2_down,
           ln_final):
    depth, D = ln_ffn1.shape
    F = w_ffn1_down.shape[1]
    bf = lambda w: w.astype(BF16)
    row = lambda v: v.reshape(1, -1)

    ffn_w = {}
    for name, gu, down in (("ffn1", w_ffn1_gu, w_ffn1_down), ("ffn2", w_ffn2_gu, w_ffn2_down)):
        for l in range(depth):
            ffn_w[name, l] = (bf(gu[l, :, :F]), bf(gu[l, :, F:]), bf(down[l]))
    even_w = {i: _prep_even(w_in_even[i], w_uq[i], w_ukv[i], w_out_even[i], D) for i in range(w_in_even.shape[0])}
    odd_w = {i: _prep_ssm(ssm_a_re[i], ssm_a_im[i], ssm_b_re[i], ssm_b_im[i], ssm_c_re[i], ssm_c_im[i],
                          ssm_log_dt[i]) + (bf(w_in_odd[i]), bf(w_glu[i])) for i in range(w_in_odd.shape[0])}
    xq_w, xo_w = bf(w_xq), bf(w_xo)

    def run(x, past, mem_k, mem_v, *, bt, tm, tq, xbt):
        B, S, _ = x.shape
        past_len = 0 if past is None else past[0].shape[2]
        rope = _rope_table(past_len + jnp.arange(S))
        flat = lambda v: v.reshape(B * S, D)
        cube = lambda v: v.reshape(B, S, D)
        lat_new, kr_new, conv_new, sre_new, sim_new = [], [], [], [], []
        for l in range(depth):
            i = l // 2
            x = cube(_ffn(flat(x), row(ln_ffn1[l]), *ffn_w["ffn1", l]))
            if l % 2 == 0:
                win, wuq, wukv, wa, wz, place = even_w[i]
                pconv = jnp.zeros((B, CONV_K - 1, D // 2), F32) if past is None else past[2][i]
                q, k, kv, lat, kr, z, nconv = _even_in(
                    x, row(ln_mix[l]), win, row(q_norm[i]), row(kv_norm[i]), wuq, wukv, conv_w[i], rope, pconv,
                    bt=bt, tm=tm)
                if past is None:
                    attn = _attention(q, k, kv, tq=tq, tk=tq)
                else:
                    P = past_len
                    kp, kvp = _past_kv(past[0][i].reshape(B * P, -1), past[1][i].reshape(B * P, -1), wukv, place)
                    attn = _attention(q, k, kv, kp.reshape(B, P, -1), kvp.reshape(B, P, -1), tq=tq, tk=256)
                lat_new.append(lat); kr_new.append(kr); conv_new.append(nconv)
                pre = (attn, z, wa, wz)
            else:
                bblk, cblk, a, w_in, wglu = odd_w[i]
                if past is None:
                    h0r = h0i = jnp.zeros((B, a.shape[0] * a.shape[-1]), F32)
                else:
                    h0r, h0i = past[3][i].reshape(B, -1), past[4][i].reshape(B, -1)
                x, hr, hi = _ssm(x, row(ln_mix[l]), w_in, bblk, a, cblk, row(ssm_d[i]), wglu, h0r, h0i)
                shape = (B, -1, SSM_STATE)
                sre_new.append(hr.reshape(shape)); sim_new.append(hi.reshape(shape))
                pre = None
            M = mem_k.shape[2]
            x = _cross(x, row(ln_cross[l]), xq_w[l], mem_k[l].reshape(B, M, D), mem_v[l].reshape(B, M, D), xo_w[l],
                       pre, bt=xbt, tm=tm)
            g_final = row(ln_final) if l == depth - 1 else None
            x = cube(_ffn(flat(x), row(ln_ffn2[l]), *ffn_w["ffn2", l], g_final))
        return (x, jnp.stack(lat_new), jnp.stack(kr_new), jnp.stack(conv_new), jnp.stack(sre_new), jnp.stack(sim_new))

    Bp, M, _ = mem_prompt.shape
    mem_k_p, mem_v_p = _memory_kv(mem_prompt.reshape(Bp * M, D), ln_mem.reshape(depth, 1, D), bf(w_xk), bf(w_xv))
    kv_shape = (depth, Bp, M, X_HEADS, D // X_HEADS)
    mem_k_p, mem_v_p = mem_k_p.reshape(kv_shape), mem_v_p.reshape(kv_shape)
    y_p, lat_p, kr_p, conv_p, sre_p, sim_p = run(x_prompt, None, mem_k_p, mem_v_p, bt=1, tm=512, tq=256, xbt=1)

    past = (cache_mla_latent, cache_mla_krope, state_conv, state_ssm_re, state_ssm_im)
    S_s = x_sample.shape[1]
    y_s, lat_s, kr_s, conv_s, sre_s, sim_s = run(x_sample, past, cache_mem_k, cache_mem_v,
                                                 bt=8, tm=S_s, tq=S_s, xbt=4)
    return (y_p, y_s, lat_p, kr_p, conv_p, sre_p, sim_p, mem_k_p, mem_v_p, lat_s, kr_s, conv_s, sre_s, sim_s)
```

```python
import functools
import math

import jax
import jax.numpy as jnp
from jax import lax
from jax.experimental import pallas as pl
from jax.experimental.pallas import tpu as pltpu

F32 = jnp.float32
BF16 = jnp.bfloat16

EPS = 1e-6
NEG_INF = -1e30
CHUNK = 64
QK_NOPE = 64
QK_ROPE = 32
V_HEAD = 64
HEAD_SLOT = 128
ROPE_BASE = 10000.0
CONV_K = 3
SSM_GROUP = 16
SSM_STATE = 64
SSM_GROUPS_PER_BLOCK = 16
X_HEADS = 4
PAST_TILE = 256
SUBLANES = 8
VMEM_LIMIT_BYTES = 56 * 1024 * 1024


def _params(*sem):
    return pltpu.CompilerParams(dimension_semantics=sem, vmem_limit_bytes=VMEM_LIMIT_BYTES)


def _rms(x, g):
    return x * lax.rsqrt(jnp.mean(x * x, axis=-1, keepdims=True) + EPS) * g


def _dot(a, b):
    return jnp.dot(a, b, preferred_element_type=F32)


def _dot_nt(a, b):
    return lax.dot_general(a, b, (((1,), (1,)), ((), ())), preferred_element_type=F32)


def _resident(shape):
    nd = len(shape)
    return pl.BlockSpec(shape, lambda *_: (0,) * nd, pipeline_mode=pl.Buffered(1))


def _ffn_body(*refs, final):
    if final:
        x_ref, g_ref, wg_ref, wu_ref, wd_ref, gf_ref, o_ref = refs
    else:
        x_ref, g_ref, wg_ref, wu_ref, wd_ref, o_ref = refs
    x = x_ref[...]
    h = _rms(x, g_ref[...]).astype(BF16)
    gate = _dot(h, wg_ref[...])
    up = _dot(h, wu_ref[...])
    a = (gate * jax.nn.sigmoid(gate) * up).astype(BF16)
    y = x + 0.5 * _dot(a, wd_ref[...])
    if final:
        y = _rms(y, gf_ref[...])
    o_ref[...] = y


def _layer_block(shape, l, col=0):
    nd = len(shape)
    return pl.BlockSpec((None,) + tuple(shape), lambda *_: (l,) + (0,) * (nd - 1) + (col,),
                        pipeline_mode=pl.Buffered(1))


def _ffn(x, g, wgu, wd, l, g_final=None, *, tm=512):
    T, D = x.shape
    F = wd.shape[1]
    tm = min(tm, T)
    final = g_final is not None
    row = pl.BlockSpec((tm, D), lambda i: (i, 0))
    in_specs = [row, _layer_block((1, D), l), _layer_block((D, F), l, 0), _layer_block((D, F), l, 1),
                _layer_block((F, D), l)]
    args = [x, g, wgu, wgu, wd]
    if final:
        in_specs.append(_resident((1, D)))
        args.append(g_final)
    return pl.pallas_call(
        functools.partial(_ffn_body, final=final),
        out_shape=jax.ShapeDtypeStruct((T, D), F32),
        grid=(T // tm,),
        in_specs=in_specs,
        out_specs=row,
        compiler_params=_params("parallel"),
        name="ffn_final" if final else "ffn",
    )(*args)


def _memkv_body(m_ref, g_ref, wk_ref, wv_ref, k_ref, v_ref):
    m = _rms(m_ref[...], g_ref[...]).astype(BF16)
    k_ref[...] = _dot(m, wk_ref[...])
    v_ref[...] = _dot(m, wv_ref[...])


def _memory_kv(mem, g, wk, wv, *, tm=512):
    R, D = mem.shape
    L = g.shape[0]
    out = jax.ShapeDtypeStruct((L, R, D), F32)
    wspec = pl.BlockSpec((None, D, D), lambda l, i: (l, 0, 0))
    ospec = pl.BlockSpec((None, tm, D), lambda l, i: (l, i, 0))
    return pl.pallas_call(
        _memkv_body,
        out_shape=(out, out),
        grid=(L, R // tm),
        in_specs=[pl.BlockSpec((tm, D), lambda l, i: (i, 0)),
                  pl.BlockSpec((None, 1, D), lambda l, i: (l, 0, 0)), wspec, wspec],
        out_specs=(ospec, ospec),
        compiler_params=_params("parallel", "parallel"),
        name="memory_kv",
    )(mem, g, wk, wv)


def _even_in_body(x_ref, g_ref, win_ref, qn_ref, kvn_ref, wuq_ref, wuk_ref, wukvt_ref, cw_ref, rope_ref, pconv_ref,
                  q_ref, k_ref, kvt_ref, lat_ref, kr_ref, z_ref, nconv_ref, ubuf, *, bt, tm):
    t = pl.program_id(1)
    D = x_ref.shape[-1]
    R = bt * tm
    ql = qn_ref.shape[-1]
    kvl = kvn_ref.shape[-1]
    cc = cw_ref.shape[-1]
    nslot = k_ref.shape[-1]

    x = x_ref[...].reshape(R, D)
    h = _rms(x, g_ref[...]).astype(BF16)
    p = _dot(h, win_ref[...])
    o = 0
    c_q = p[:, o:o + ql]; o += ql
    c_kv = p[:, o:o + kvl]; o += kvl
    gate_b = p[:, o:o + cc]; o += cc
    gate_c = p[:, o:o + cc]; o += cc
    v_in = p[:, o:o + cc]; o += cc
    k_slot = p[:, o:o + HEAD_SLOT]; o += HEAD_SLOT
    k_slot_sw = p[:, o:o + HEAD_SLOT]

    rope = rope_ref[...]
    reps = nslot // HEAD_SLOT
    cos_q = jnp.tile(rope[:, 0:HEAD_SLOT], (1, reps))[None]
    sin_q = jnp.tile(rope[:, HEAD_SLOT:2 * HEAD_SLOT], (1, reps))[None]
    cos_k = rope[:, 2 * HEAD_SLOT:3 * HEAD_SLOT][None]
    sin_k = rope[:, 3 * HEAD_SLOT:4 * HEAD_SLOT][None]

    q2 = _dot(_rms(c_q, qn_ref[...]).astype(BF16), wuq_ref[...])
    q_rot = q2[:, :nslot].reshape(bt, tm, nslot) * cos_q + q2[:, nslot:].reshape(bt, tm, nslot) * sin_q
    q_ref[...] = q_rot.astype(BF16)

    latent = _rms(c_kv, kvn_ref[...])
    lat_ref[...] = latent.reshape(bt, tm, kvl)
    lat_b = latent.astype(BF16)
    tkv = kvt_ref.shape[-1]
    for b in range(bt):
        for j in range(tm // tkv):
            r0 = b * tm + j * tkv
            kvt_ref[b, j] = _dot_nt(wukvt_ref[...], lat_b[r0:r0 + tkv]).astype(BF16)

    k_rot = k_slot.reshape(bt, tm, HEAD_SLOT) * cos_k + k_slot_sw.reshape(bt, tm, HEAD_SLOT) * sin_k
    kr_ref[...] = k_rot[:, :, QK_NOPE:QK_NOPE + QK_ROPE]
    k_all = _dot(lat_b, wuk_ref[...]).reshape(bt, tm, nslot) + jnp.tile(k_rot, (1, 1, reps))
    k_ref[...] = k_all.astype(BF16)

    u = (gate_c * v_in).reshape(bt, tm, cc)
    ubuf[:, pl.ds(SUBLANES, tm), :] = u

    @pl.when(t == 0)
    def _():
        ubuf[:, pl.ds(SUBLANES - (CONV_K - 1), CONV_K - 1), :] = pconv_ref[...]

    cw = cw_ref[...]
    conv = (cw[0:1][None] * ubuf[:, pl.ds(SUBLANES - 2, tm), :]
            + cw[1:2][None] * ubuf[:, pl.ds(SUBLANES - 1, tm), :]
            + cw[2:3][None] * u)
    z_ref[...] = (gate_b.reshape(bt, tm, cc) * conv).astype(BF16)
    nconv_ref[...] = u[:, tm - (CONV_K - 1):, :]
    ubuf[:, pl.ds(0, SUBLANES), :] = ubuf[:, pl.ds(tm, SUBLANES), :]


def _even_in(x, g, win, qn, kvn, wuq, wuk, wukvt, cw, rope, pconv, *, bt, tm, tkv):
    B, S, D = x.shape
    nslot = wuk.shape[1]
    kvl = kvn.shape[-1]
    cc = cw.shape[-1]

    def act(w, dt):
        return jax.ShapeDtypeStruct((B, S, w), dt), pl.BlockSpec((bt, tm, w), lambda b, t: (b, t, 0))

    kvt = (jax.ShapeDtypeStruct((B, S // tkv, nslot, tkv), BF16),
           pl.BlockSpec((bt, tm // tkv, nslot, tkv), lambda b, t: (b, t, 0, 0)))
    outs = [act(nslot, BF16), act(nslot, BF16), kvt, act(kvl, F32), act(QK_ROPE, F32), act(cc, BF16),
            (jax.ShapeDtypeStruct((B, CONV_K - 1, cc), F32), pl.BlockSpec((bt, CONV_K - 1, cc), lambda b, t: (b, 0, 0)))]
    return pl.pallas_call(
        functools.partial(_even_in_body, bt=bt, tm=tm),
        out_shape=tuple(o[0] for o in outs),
        grid=(B // bt, S // tm),
        in_specs=[pl.BlockSpec((bt, tm, D), lambda b, t: (b, t, 0)),
                  _resident(g.shape), _resident(win.shape), _resident(qn.shape), _resident(kvn.shape),
                  _resident(wuq.shape), _resident(wuk.shape), _resident(wukvt.shape), _resident(cw.shape),
                  pl.BlockSpec((tm, rope.shape[1]), lambda b, t: (t, 0)),
                  pl.BlockSpec((bt, CONV_K - 1, cc), lambda b, t: (b, 0, 0))],
        out_specs=tuple(o[1] for o in outs),
        scratch_shapes=[pltpu.VMEM((bt, tm + SUBLANES, cc), F32)],
        compiler_params=_params("parallel", "arbitrary"),
        name="even_in",
    )(x, g, win, qn, kvn, wuq, wuk, wukvt, cw, rope, pconv)


def _past_kv_body(lat_ref, kr_ref, wuk_ref, wukvt_ref, place_ref, k_ref, kvt_ref):
    lat_b = lat_ref[...].astype(BF16)
    tkv = kvt_ref.shape[-1]
    for j in range(kvt_ref.shape[0]):
        kvt_ref[j] = _dot_nt(wukvt_ref[...], lat_b[j * tkv:(j + 1) * tkv]).astype(BF16)
    k_ref[...] = (_dot(lat_b, wuk_ref[...]) + _dot(kr_ref[...].astype(BF16), place_ref[...])).astype(BF16)


def _past_kv(lat, kr, wuk, wukvt, place, *, tkv, tm=512):
    R, kvl = lat.shape
    nslot = wuk.shape[1]
    return pl.pallas_call(
        _past_kv_body,
        out_shape=(jax.ShapeDtypeStruct((R, nslot), BF16), jax.ShapeDtypeStruct((R // tkv, nslot, tkv), BF16)),
        grid=(R // tm,),
        in_specs=[pl.BlockSpec((tm, kvl), lambda i: (i, 0)), pl.BlockSpec((tm, QK_ROPE), lambda i: (i, 0)),
                  _resident(wuk.shape), _resident(wukvt.shape), _resident(place.shape)],
        out_specs=(pl.BlockSpec((tm, nslot), lambda i: (i, 0)),
                   pl.BlockSpec((tm // tkv, nslot, tkv), lambda i: (i, 0, 0))),
        compiler_params=_params("parallel"),
        name="past_kv",
    )(lat, kr, wuk, wukvt, place)


def _attn_body(q_ref, kf_ref, vf_ref, kt_ref, vt_ref, o_ref, m_s, l_s, acc_s, s_s, p_s, *, tq, tk, n_full_base,
               heads):
    i = pl.program_id(1)
    n_full = n_full_base + i * (tq // tk)
    m_s[...] = jnp.full(m_s.shape, NEG_INF, F32)
    l_s[...] = jnp.zeros(l_s.shape, F32)
    acc_s[...] = jnp.zeros(acc_s.shape, F32)

    def advance(k_ref, vt_ref, visible):
        nk = k_ref.shape[0]
        slots = [slice(h * HEAD_SLOT, (h + 1) * HEAD_SLOT) for h in range(heads)]
        m_new, alpha = [], []
        for h, sl in enumerate(slots):
            s = _dot_nt(k_ref[:, sl], q_ref[:, sl])
            if visible is not None:
                s = jnp.where(visible, s, NEG_INF)
            s_s[h, :nk] = s
            m = m_s[h]
            m_new.append(jnp.maximum(m, jnp.max(s, axis=0, keepdims=True)))
            alpha.append(jnp.exp2(m - m_new[h]))
            m_s[h] = m_new[h]
        for h in range(heads):
            p = jnp.exp2(s_s[h, :nk] - m_new[h])
            l_s[h] = alpha[h] * l_s[h] + jnp.sum(p, axis=0, keepdims=True)
            p_s[h, :nk] = p.astype(BF16)
        for h, sl in enumerate(slots):
            acc_s[h] = alpha[h] * acc_s[h] + _dot(vt_ref[sl, :], p_s[h, :nk])

    def step(j, carry):
        advance(kf_ref.at[pl.ds(pl.multiple_of(j * tk, tk), tk)], vf_ref.at[j], None)
        return carry

    lax.fori_loop(0, n_full, step, 0)
    key_chunk = lax.broadcasted_iota(jnp.int32, (tq, tq), 0) // CHUNK
    query_chunk = lax.broadcasted_iota(jnp.int32, (tq, tq), 1) // CHUNK
    advance(kt_ref, vt_ref, key_chunk <= query_chunk)
    for h in range(heads):
        o_ref[:, h * HEAD_SLOT:(h + 1) * HEAD_SLOT] = (acc_s[h] / l_s[h]).T.astype(BF16)


def _attention(q, k, kvt, k_past=None, kvt_past=None, *, tq, tk):
    B, S, W = q.shape
    heads = W // HEAD_SLOT
    if k_past is None:
        k_full, v_full, n_full_base = k, kvt, 0
    else:
        k_full, v_full, n_full_base = k_past, kvt_past, k_past.shape[1] // tk
    tile = pl.BlockSpec((None, tq, W), lambda b, i: (b, i, 0))
    k_all = pl.BlockSpec((None,) + k_full.shape[1:], lambda b, i: (b, 0, 0))
    v_all = pl.BlockSpec((None,) + v_full.shape[1:], lambda b, i: (b, 0, 0, 0))
    v_tile = pl.BlockSpec((None, None, W, tq), lambda b, i: (b, i, 0, 0))
    return pl.pallas_call(
        functools.partial(_attn_body, tq=tq, tk=tk, n_full_base=n_full_base, heads=heads),
        out_shape=jax.ShapeDtypeStruct((B, S, W), BF16),
        grid=(B, S // tq),
        in_specs=[tile, k_all, v_all, tile, v_tile],
        out_specs=tile,
        scratch_shapes=[pltpu.VMEM((heads, 1, tq), F32), pltpu.VMEM((heads, 1, tq), F32),
                        pltpu.VMEM((heads, HEAD_SLOT, tq), F32),
                        pltpu.VMEM((heads, max(tk, tq), tq), F32), pltpu.VMEM((heads, max(tk, tq), tq), BF16)],
        compiler_params=_params("parallel", "parallel"),
        name="attention",
    )(q, k_full, v_full, k, kvt)


def _cross_body(*refs, bt, tm, pre):
    if pre:
        x_ref, at_ref, z_ref, wa_ref, wz_ref, g_ref, wq_ref, mk_ref, mv_ref, wo_ref, o_ref, oh_ref = refs
    else:
        x_ref, g_ref, wq_ref, mk_ref, mv_ref, wo_ref, o_ref, oh_ref = refs
    D = x_ref.shape[-1]
    R = bt * tm
    x = x_ref[...].reshape(R, D)
    if pre:
        x = (x + _dot(at_ref[...].reshape(R, at_ref.shape[-1]), wa_ref[...])
             + _dot(z_ref[...].reshape(R, z_ref.shape[-1]), wz_ref[...]))
    h = _rms(x, g_ref[...]).astype(BF16)
    q = _dot(h, wq_ref[...]).astype(BF16)
    hd = D // X_HEADS
    scale = hd ** -0.5
    for b in range(bt):
        rows = slice(b * tm, (b + 1) * tm)
        for hh in range(X_HEADS):
            cols = slice(hh * hd, (hh + 1) * hd)
            s = _dot_nt(q[rows, cols], mk_ref[b, :, cols].astype(BF16)) * scale
            e = jnp.exp(s - jnp.max(s, axis=-1, keepdims=True))
            p = e / jnp.sum(e, axis=-1, keepdims=True)
            oh_ref[rows, cols] = _dot(p.astype(BF16), mv_ref[b, :, cols].astype(BF16)).astype(BF16)
    o_ref[...] = (x + _dot(oh_ref[...], wo_ref[...])).reshape(bt, tm, D)


def _cross(x, g, wq, mk, mv, wo, l, pre=None, *, bt, tm):
    B, S, D = x.shape
    M = mk.shape[2]
    xs = pl.BlockSpec((bt, tm, D), lambda b, t: (b, t, 0))
    ms = pl.BlockSpec((None, bt, M, D), lambda b, t: (l, b, 0, 0))
    in_specs, args = [xs], [x]
    if pre is not None:
        at, z, wa, wz = pre
        in_specs += [pl.BlockSpec((bt, tm, at.shape[-1]), lambda b, t: (b, t, 0)),
                     pl.BlockSpec((bt, tm, z.shape[-1]), lambda b, t: (b, t, 0)),
                     _resident(wa.shape), _resident(wz.shape)]
        args += [at, z, wa, wz]
    in_specs += [_layer_block((1, D), l), _layer_block((D, D), l), ms, ms, _layer_block((D, D), l)]
    args += [g, wq, mk, mv, wo]
    return pl.pallas_call(
        functools.partial(_cross_body, bt=bt, tm=tm, pre=pre is not None),
        out_shape=jax.ShapeDtypeStruct((B, S, D), F32),
        grid=(B // bt, S // tm),
        in_specs=in_specs,
        out_specs=xs,
        scratch_shapes=[pltpu.VMEM((bt * tm, D), BF16)],
        compiler_params=_params("parallel", "parallel"),
        name="mix_out_cross" if pre is not None else "cross",
    )(*args)


def _ssm_body(x_ref, g_ref, win_ref, bblk_ref, a_ref, cblk_ref, d_ref, wglu_ref, h0r_ref, h0i_ref,
              o_ref, hlr_ref, hli_ref, st_re, st_im, xt, bu, hs, y, *, lt):
    tt = pl.program_id(1)
    D = x_ref.shape[-1]
    nb = bblk_ref.shape[0]
    gw = bblk_ref.shape[1]
    sw = bblk_ref.shape[2] // 2

    @pl.when(tt == 0)
    def _():
        st_re[...] = h0r_ref[...]
        st_im[...] = h0i_ref[...]

    for t in range(lt):
        xt[pl.ds(t * SUBLANES, SUBLANES), :] = x_ref[:, t, :]
    x = xt[...]
    h = _rms(x, g_ref[...]).astype(BF16)
    u = _dot(h, win_ref[...])
    ub = u.astype(BF16)
    for gb in range(nb):
        bu_g, hs_g = bu.at[gb % 2], hs.at[gb % 2]
        bu_g[...] = _dot(ub[:, gb * gw:(gb + 1) * gw], bblk_ref[gb])
        a_re = jnp.broadcast_to(a_ref[gb, 0], (SUBLANES, sw))
        a_im = jnp.broadcast_to(a_ref[gb, 1], (SUBLANES, sw))
        cols = slice(gb * sw, (gb + 1) * sw)
        hr, hi = st_re[:, cols], st_im[:, cols]
        for t in range(lt):
            rows = pl.ds(t * SUBLANES, SUBLANES)
            hr, hi = (a_re * hr - a_im * hi + bu_g[rows, 0:sw],
                      a_re * hi + a_im * hr + bu_g[rows, sw:2 * sw])
            hs_g[rows, 0:sw] = hr
            hs_g[rows, sw:2 * sw] = hi
        st_re[:, cols] = hr
        st_im[:, cols] = hi
        y[:, gb * gw:(gb + 1) * gw] = _dot(hs_g[...].astype(BF16), cblk_ref[gb])

    yy = y[...] + d_ref[...] * u
    vg = _dot(jax.nn.gelu(yy).astype(BF16), wglu_ref[...])
    xt[...] = x + vg[:, :D] * jax.nn.sigmoid(vg[:, D:])
    for t in range(lt):
        o_ref[:, t, :] = xt[pl.ds(t * SUBLANES, SUBLANES), :]
    hlr_ref[...] = st_re[...]
    hli_ref[...] = st_im[...]


def _ssm(x, g, win, bblk, a, cblk, d, wglu, h0r, h0i, *, lt=CHUNK):
    B, S, D = x.shape
    NS = h0r.shape[1]
    R = SUBLANES * lt
    sw2 = bblk.shape[2]
    xs = pl.BlockSpec((SUBLANES, lt, D), lambda b, t: (b, t, 0))
    ss = pl.BlockSpec((SUBLANES, NS), lambda b, t: (b, 0))
    st = jax.ShapeDtypeStruct((B, NS), F32)
    return pl.pallas_call(
        functools.partial(_ssm_body, lt=lt),
        out_shape=(jax.ShapeDtypeStruct((B, S, D), F32), st, st),
        grid=(B // SUBLANES, S // lt),
        in_specs=[xs, _resident(g.shape), _resident(win.shape), _resident(bblk.shape), _resident(a.shape),
                  _resident(cblk.shape), _resident(d.shape), _resident(wglu.shape), ss, ss],
        out_specs=(xs, ss, ss),
        scratch_shapes=[pltpu.VMEM((SUBLANES, NS), F32), pltpu.VMEM((SUBLANES, NS), F32),
                        pltpu.VMEM((R, D), F32), pltpu.VMEM((2, R, sw2), F32), pltpu.VMEM((2, R, sw2), F32),
                        pltpu.VMEM((R, D), F32)],
        compiler_params=_params("parallel", "arbitrary"),
        name="ssm",
    )(x, g, win, bblk, a, cblk, d, wglu, h0r, h0i)


def _rope_swap(w):
    half = w.shape[-1] // 2
    return jnp.concatenate([-w[..., half:], w[..., :half]], axis=-1)


def _in_slot(w):
    return jnp.pad(w, ((0, 0), (QK_NOPE, HEAD_SLOT - QK_NOPE - QK_ROPE)))


def _prep_even(w_in, w_uq, w_ukv, w_out, d_model):
    ql = kvl = d_model // 4
    cc = d_model // 2
    heads = d_model // 128
    o = 0
    w_cq = w_in[:, o:o + ql]; o += ql
    w_ckv = w_in[:, o:o + kvl]; o += kvl
    w_kr = w_in[:, o:o + QK_ROPE]; o += QK_ROPE
    w_gb = w_in[:, o:o + cc]; o += cc
    w_gc = w_in[:, o:o + cc]; o += cc
    w_v = w_in[:, o:o + cc]
    win = jnp.concatenate([w_cq, w_ckv, w_gb, w_gc, w_v, _in_slot(w_kr), _in_slot(_rope_swap(w_kr))], axis=1)

    wq = w_uq.reshape(ql, heads, QK_NOPE + QK_ROPE)
    w_nope, w_rope = wq[..., :QK_NOPE], wq[..., QK_NOPE:]
    pad = jnp.zeros((ql, heads, HEAD_SLOT - QK_NOPE - QK_ROPE), w_uq.dtype)
    q_plain = jnp.concatenate([w_nope, w_rope, pad], axis=-1).reshape(ql, heads * HEAD_SLOT)
    q_swap = jnp.concatenate([jnp.zeros_like(w_nope), _rope_swap(w_rope), pad], axis=-1).reshape(ql, heads * HEAD_SLOT)
    wuq = jnp.concatenate([q_plain, q_swap], axis=1)

    wa = w_out[:heads * V_HEAD].reshape(heads, V_HEAD, d_model)
    wa = jnp.concatenate([jnp.zeros_like(wa), wa], axis=1).reshape(heads * HEAD_SLOT, d_model)
    wz = w_out[heads * V_HEAD:]
    place = jnp.tile(_in_slot(jnp.eye(QK_ROPE, dtype=F32)), (1, heads))
    lane = jnp.arange(heads * HEAD_SLOT) % HEAD_SLOT
    wuk = jnp.where(lane[None, :] < QK_NOPE, w_ukv, 0.0)
    return (win.astype(BF16), wuq.astype(BF16), wuk.astype(BF16), w_ukv.T.astype(BF16), wa.astype(BF16),
            wz.astype(BF16), place.astype(BF16))


def _rope_table(pos):
    scale = (QK_NOPE + QK_ROPE) ** -0.5 * math.log2(math.e)
    inv = ROPE_BASE ** (-jnp.arange(0, QK_ROPE, 2, dtype=F32) / QK_ROPE)
    ang = pos.astype(F32)[:, None] * inv[None, :]
    cos2 = jnp.tile(jnp.cos(ang), (1, 2))
    sin2 = jnp.tile(jnp.sin(ang), (1, 2))
    S = pos.shape[0]
    tail = jnp.zeros((S, HEAD_SLOT - QK_NOPE - QK_ROPE), F32)
    head0 = jnp.zeros((S, QK_NOPE), F32)
    cos_k = jnp.concatenate([head0, cos2, tail], axis=1)
    sin_k = jnp.concatenate([head0, sin2, tail], axis=1)
    cos_q = jnp.concatenate([jnp.ones((S, QK_NOPE), F32), cos2, tail], axis=1) * scale
    return jnp.concatenate([cos_q, sin_k * scale, cos_k, sin_k], axis=1)


def _prep_ssm(a_re, a_im, b_re, b_im, c_re, c_im, log_dt):
    G, P = a_re.shape
    gpb = SSM_GROUPS_PER_BLOCK
    nb = G // gpb
    dt = jnp.exp(log_dt)[:, None]
    mag = jnp.exp(a_re * dt)
    ab_re = mag * jnp.cos(a_im * dt)
    ab_im = mag * jnp.sin(a_im * dt)
    den = a_re * a_re + a_im * a_im
    q_re = ((ab_re - 1.0) * a_re + ab_im * a_im) / den
    q_im = (ab_im * a_re - (ab_re - 1.0) * a_im) / den
    bb_re = q_re[..., None] * b_re - q_im[..., None] * b_im
    bb_im = q_re[..., None] * b_im + q_im[..., None] * b_re
    eye = jnp.eye(gpb, dtype=F32)

    def b_block(bb):
        bb = bb.reshape(nb, gpb, P, SSM_GROUP)
        return jnp.einsum("ngpc,gh->ngchp", bb, eye).reshape(nb, gpb * SSM_GROUP, gpb * P)

    def c_block(c):
        c = c.reshape(nb, gpb, SSM_GROUP, P)
        return jnp.einsum("ngcp,gh->ngphc", c, eye).reshape(nb, gpb * P, gpb * SSM_GROUP)

    bblk = jnp.concatenate([b_block(bb_re), b_block(bb_im)], axis=2)
    cblk = jnp.concatenate([c_block(c_re), c_block(-c_im)], axis=1)
    a = jnp.stack([ab_re.reshape(nb, 1, gpb * P), ab_im.reshape(nb, 1, gpb * P)], axis=1)
    return bblk.astype(BF16), cblk.astype(BF16), a


def kernel(x_prompt, x_sample, cache_mla_latent, cache_mla_krope, state_conv, state_ssm_re, state_ssm_im,
           cache_mem_k, cache_mem_v, mem_prompt,
           ln_ffn1, w_ffn1_gu, w_ffn1_down, ln_mix, w_in_even, q_norm, kv_norm, w_uq, w_ukv, conv_w,
           w_out_even, w_in_odd, ssm_a_re, ssm_a_im, ssm_b_re, ssm_b_im, ssm_c_re, ssm_c_im, ssm_log_dt,
           ssm_d, w_glu, ln_cross, ln_mem, w_xq, w_xk, w_xv, w_xo, ln_ffn2, w_ffn2_gu, w_ffn2_down,
           ln_final):
    depth, D = ln_ffn1.shape
    bf = lambda w: w.astype(BF16)
    row = lambda v: v.reshape(1, -1)
    gains = lambda v: v.reshape(depth, 1, D)

    ffn_w = {"ffn1": (gains(ln_ffn1), bf(w_ffn1_gu), bf(w_ffn1_down)),
             "ffn2": (gains(ln_ffn2), bf(w_ffn2_gu), bf(w_ffn2_down))}
    even_w = {i: _prep_even(w_in_even[i], w_uq[i], w_ukv[i], w_out_even[i], D) for i in range(w_in_even.shape[0])}
    odd_w = {i: _prep_ssm(ssm_a_re[i], ssm_a_im[i], ssm_b_re[i], ssm_b_im[i], ssm_c_re[i], ssm_c_im[i],
                          ssm_log_dt[i]) + (bf(w_in_odd[i]), bf(w_glu[i])) for i in range(w_in_odd.shape[0])}
    g_cross, xq_w, xo_w = gains(ln_cross), bf(w_xq), bf(w_xo)

    def run(x, past, mem_k, mem_v, *, bt, tm, tq, xbt):
        B, S, _ = x.shape
        past_len = 0 if past is None else past[0].shape[2]
        rope = _rope_table(past_len + jnp.arange(S))
        flat = lambda v: v.reshape(B * S, D)
        cube = lambda v: v.reshape(B, S, D)
        mem_k = mem_k.reshape(depth, B, -1, D)
        mem_v = mem_v.reshape(depth, B, -1, D)
        lat_new, kr_new, conv_new, sre_new, sim_new = [], [], [], [], []
        for l in range(depth):
            i = l // 2
            x = cube(_ffn(flat(x), *ffn_w["ffn1"], l))
            if l % 2 == 0:
                win, wuq, wuk, wukvt, wa, wz, place = even_w[i]
                pconv = jnp.zeros((B, CONV_K - 1, D // 2), F32) if past is None else past[2][i]
                q, k, kvt, lat, kr, z, nconv = _even_in(
                    x, row(ln_mix[l]), win, row(q_norm[i]), row(kv_norm[i]), wuq, wuk, wukvt, conv_w[i], rope, pconv,
                    bt=bt, tm=tm, tkv=tq)
                if past is None:
                    attn = _attention(q, k, kvt, tq=tq, tk=tq)
                else:
                    P, W = past_len, k.shape[-1]
                    kp, kvtp = _past_kv(past[0][i].reshape(B * P, -1), past[1][i].reshape(B * P, -1), wuk, wukvt,
                                        place, tkv=PAST_TILE)
                    attn = _attention(q, k, kvt, kp.reshape(B, P, W), kvtp.reshape(B, P // PAST_TILE, W, PAST_TILE),
                                      tq=tq, tk=PAST_TILE)
                lat_new.append(lat); kr_new.append(kr); conv_new.append(nconv)
                pre = (attn, z, wa, wz)
            else:
                bblk, cblk, a, w_in, wglu = odd_w[i]
                if past is None:
                    h0r = h0i = jnp.zeros((B, a.shape[0] * a.shape[-1]), F32)
                else:
                    h0r, h0i = past[3][i].reshape(B, -1), past[4][i].reshape(B, -1)
                x, hr, hi = _ssm(x, row(ln_mix[l]), w_in, bblk, a, cblk, row(ssm_d[i]), wglu, h0r, h0i)
                shape = (B, -1, SSM_STATE)
                sre_new.append(hr.reshape(shape)); sim_new.append(hi.reshape(shape))
                pre = None
            x = _cross(x, g_cross, xq_w, mem_k, mem_v, xo_w, l, pre, bt=xbt, tm=tm)
            g_final = row(ln_final) if l == depth - 1 else None
            x = cube(_ffn(flat(x), *ffn_w["ffn2"], l, g_final))
        return (x, jnp.stack(lat_new), jnp.stack(kr_new), jnp.stack(conv_new), jnp.stack(sre_new), jnp.stack(sim_new))

    Bp, M, _ = mem_prompt.shape
    mem_k_p, mem_v_p = _memory_kv(mem_prompt.reshape(Bp * M, D), gains(ln_mem), bf(w_xk), bf(w_xv))
    kv_shape = (depth, Bp, M, X_HEADS, D // X_HEADS)
    mem_k_p, mem_v_p = mem_k_p.reshape(kv_shape), mem_v_p.reshape(kv_shape)
    y_p, lat_p, kr_p, conv_p, sre_p, sim_p = run(x_prompt, None, mem_k_p, mem_v_p, bt=1, tm=512, tq=256, xbt=1)

    past = (cache_mla_latent, cache_mla_krope, state_conv, state_ssm_re, state_ssm_im)
    S_s = x_sample.shape[1]
    y_s, lat_s, kr_s, conv_s, sre_s, sim_s = run(x_sample, past, cache_mem_k, cache_mem_v,
                                                 bt=8, tm=S_s, tq=S_s, xbt=4)
    return (y_p, y_s, lat_p, kr_p, conv_p, sre_p, sim_p, mem_k_p, mem_v_p, lat_s, kr_s, conv_s, sre_s, sim_s)
```
